```python
import math
import jax
import jax.numpy as jnp
from jax import lax
import numpy as np

D_MODEL = 1024
BATCH = 4
SEQ = 4096
DEPTH = 4

CHUNK = 64
N_META = 16
Q_BLOCK = 128

A_HEADS = 4
A_DH = 64
A_DV = 2 * A_DH
A_QK = A_HEADS * 2 * A_DH
A_W = A_HEADS * A_DV
B_HEADS = 8
B_DH = 64
B_W = B_HEADS * B_DH
EVEN_W = A_W + B_W
EVEN_SPLITS = (A_QK, A_QK, A_W, A_W, B_W, B_W, B_W, B_W)
EVEN_IN = 2 * A_QK + 2 * A_W + 4 * B_W

C_HEADS = 8
C_NOPE = 128
C_ROPE = 64
C_DV = 128
C_QK = C_NOPE + C_ROPE
Q_LORA = 512
KV_LORA = 256
C_W = C_HEADS * C_DV
ODD_SPLITS = (Q_LORA, KV_LORA, C_ROPE, C_W)
ODD_IN = Q_LORA + KV_LORA + C_ROPE + C_W

ROPE_THETA = 10000.0
LN_EPS = 1e-5
RMS_EPS = 1e-6
ALPHA = (2.0 * DEPTH) ** 0.25
BETA = (8.0 * DEPTH) ** -0.25
N_EVEN = (DEPTH + 1) // 2
N_ODD = DEPTH // 2

kernel_name = "hybrid_diff_stickbreak_mla_streaming"


def layer_norm(x, g, b):
    xf = x.astype(jnp.float32)
    mu = jnp.mean(xf, axis=-1, keepdims=True)
    var = jnp.mean(jnp.square(xf - mu), axis=-1, keepdims=True)
    return ((xf - mu) * lax.rsqrt(var + LN_EPS) * g.astype(jnp.float32) + b.astype(jnp.float32)).astype(x.dtype)


def rms_norm(x, g):
    xf = x.astype(jnp.float32)
    return (xf * lax.rsqrt(jnp.mean(xf * xf, axis=-1, keepdims=True) + RMS_EPS) * g.astype(jnp.float32)).astype(x.dtype)


def chunk_id(pos):
    return jnp.where(pos < N_META, 0, 1 + (pos - N_META) // CHUNK)


def split_cols(h, sizes):
    idx = np.cumsum(sizes)[:-1].tolist()
    return jnp.split(h, idx, axis=-1)


def rope(x, pos):
    half = x.shape[-1] // 2
    inv_freq = ROPE_THETA ** (-jnp.arange(half, dtype=jnp.float32) / half)
    ang = pos.astype(jnp.float32)[:, None] * inv_freq[None, :]
    cos = jnp.cos(ang)[None, :, None, :]
    sin = jnp.sin(ang)[None, :, None, :]
    x1 = x[..., :half].astype(jnp.float32)
    x2 = x[..., half:].astype(jnp.float32)
    return jnp.concatenate([x1 * cos - x2 * sin, x2 * cos + x1 * sin], axis=-1).astype(x.dtype)


def sweep_query_blocks(fn, q):
    L = q.shape[1]
    n_blocks = L // Q_BLOCK

    def body(i):
        q0 = i * Q_BLOCK
        return fn(q0, lax.dynamic_slice_in_dim(q, q0, Q_BLOCK, axis=1))

    out = lax.map(body, jnp.arange(n_blocks))
    out = jnp.moveaxis(out, 0, 1)
    return out.reshape((out.shape[0], L) + out.shape[3:])


def even_layer(x, pos, cid, layer_idx, w_in, w_out, lam_q1, lam_k1, lam_q2, lam_k2, subln_g):
    bsz, L, _ = x.shape
    qa, ka, va, ga, qs, ks, vs, gs = split_cols(x @ w_in, EVEN_SPLITS)

    qa = qa.reshape(bsz, L, A_HEADS, 2, A_DH)
    ka = ka.reshape(bsz, L, A_HEADS, 2, A_DH)
    va = va.reshape(bsz, L, A_HEADS, A_DV)
    lam_init = 0.8 - 0.6 * math.exp(-0.3 * layer_idx)
    lam = (jnp.exp(jnp.sum(lam_q1.astype(jnp.float32) * lam_k1.astype(jnp.float32)))
           - jnp.exp(jnp.sum(lam_q2.astype(jnp.float32) * lam_k2.astype(jnp.float32)))
           + lam_init)
    slopes = jnp.asarray(2.0 ** (-8.0 * np.arange(1, A_HEADS + 1) / A_HEADS), dtype=jnp.float32)

    def diff_block(q0, qb):
        tq = q0 + jnp.arange(Q_BLOCK)
        s = jnp.einsum('bqhmd,bkhmd->bhmqk', qb, ka).astype(jnp.float32) * (A_DH ** -0.5)
        dist = jnp.abs(tq[:, None] - pos[None, :]).astype(jnp.float32)
        s = s - slopes[None, :, None, None, None] * dist
        visible = cid[None, :] <= chunk_id(tq)[:, None]
        p = jax.nn.softmax(jnp.where(visible, s, -jnp.inf), axis=-1)
        w = p[:, :, 0] - lam * p[:, :, 1]
        return jnp.einsum('bhqk,bkhd->bqhd', w.astype(va.dtype), va)

    oa = sweep_query_blocks(diff_block, qa)
    oa = rms_norm(oa, subln_g) * (1.0 - lam_init)
    ya = oa.reshape(bsz, L, A_W) * jax.nn.silu(ga)

    qs = qs.reshape(bsz, L, B_HEADS, B_DH)
    ks = ks.reshape(bsz, L, B_HEADS, B_DH)
    vs = vs.reshape(bsz, L, B_HEADS, B_DH)

    def sb_block(q0, qb):
        tq = q0 + jnp.arange(Q_BLOCK)
        z = jnp.einsum('bqhd,bkhd->bhqk', qb, ks).astype(jnp.float32) * (B_DH ** -0.5)
        earlier = pos[None, :] < tq[:, None]
        log_skip = jnp.where(earlier, jax.nn.log_sigmoid(-z), 0.0)
        log_pass = lax.cumsum(log_skip, axis=3, reverse=True) - log_skip
        a = jnp.where(earlier, jnp.exp(jax.nn.log_sigmoid(z) + log_pass), 0.0)
        return jnp.einsum('bhqk,bkhd->bqhd', a.astype(vs.dtype), vs)

    ys = sweep_query_blocks(sb_block, qs).reshape(bsz, L, B_W) * jax.nn.silu(gs)

    return jnp.concatenate([ya, ys], axis=-1) @ w_out


def odd_layer(x, pos, cid, w_in, q_norm, w_uq, kv_norm, w_ukv, w_out):
    bsz, L, _ = x.shape
    cq, ckv, k_pe, g = split_cols(x @ w_in, ODD_SPLITS)

    q = (rms_norm(cq, q_norm) @ w_uq).reshape(bsz, L, C_HEADS, C_QK)
    q = jnp.concatenate([q[..., :C_NOPE], rope(q[..., C_NOPE:], pos)], axis=-1)
    kv = (rms_norm(ckv, kv_norm) @ w_ukv).reshape(bsz, L, C_HEADS, C_NOPE + C_DV)
    k_pe = rope(k_pe[:, :, None, :], pos)
    k = jnp.concatenate([kv[..., :C_NOPE], jnp.broadcast_to(k_pe, (bsz, L, C_HEADS, C_ROPE))], axis=-1)
    v = kv[..., C_NOPE:]

    def mla_block(q0, qb):
        tq = q0 + jnp.arange(Q_BLOCK)
        s = jnp.einsum('bqhd,bkhd->bhqk', qb, k).astype(jnp.float32) * (C_QK ** -0.5)
        visible = cid[None, :] <= chunk_id(tq)[:, None]
        p = jax.nn.softmax(jnp.where(visible, s, -jnp.inf), axis=-1)
        return jnp.einsum('bhqk,bkhd->bqhd', p.astype(v.dtype), v)

    o = sweep_query_blocks(mla_block, q).reshape(bsz, L, C_W) * jax.nn.silu(g)
    return o @ w_out


def setup_inputs(seed: int = 0) -> dict:
    key = jax.random.key(seed)
    k = jax.random.split(key, 18)
    n = jax.random.normal
    f32 = jnp.float32
    return {
        "x": n(k[0], (BATCH, SEQ, D_MODEL), f32),
        "meta": n(k[1], (N_META, D_MODEL), f32),
        "ln_g": 1.0 + 0.02 * n(k[2], (DEPTH, D_MODEL), f32),
        "ln_b": 0.02 * n(k[3], (DEPTH, D_MODEL), f32),
        "ev_w_in": n(k[4], (N_EVEN, D_MODEL, EVEN_IN), f32) * D_MODEL ** -0.5,
        "ev_w_out": n(k[5], (N_EVEN, EVEN_W, D_MODEL), f32) * (EVEN_W ** -0.5 * BETA),
        "ev_lam_q1": 0.1 * n(k[6], (N_EVEN, A_DH), f32),
        "ev_lam_k1": 0.1 * n(k[7], (N_EVEN, A_DH), f32),
        "ev_lam_q2": 0.1 * n(k[8], (N_EVEN, A_DH), f32),
        "ev_lam_k2": 0.1 * n(k[9], (N_EVEN, A_DH), f32),
        "ev_subln": 1.0 + 0.02 * n(k[10], (N_EVEN, A_DV), f32),
        "od_w_in": n(k[11], (N_ODD, D_MODEL, ODD_IN), f32) * D_MODEL ** -0.5,
        "od_q_norm": 1.0 + 0.02 * n(k[12], (N_ODD, Q_LORA), f32),
        "od_w_uq": n(k[13], (N_ODD, Q_LORA, C_HEADS * C_QK), f32) * Q_LORA ** -0.5,
        "od_kv_norm": 1.0 + 0.02 * n(k[14], (N_ODD, KV_LORA), f32),
        "od_w_ukv": n(k[15], (N_ODD, KV_LORA, C_HEADS * (C_NOPE + C_DV)), f32) * KV_LORA ** -0.5,
        "od_w_out": n(k[16], (N_ODD, C_W, D_MODEL), f32) * (C_W ** -0.5 * BETA),
    }


def reference(x, meta, ln_g, ln_b, ev_w_in, ev_w_out, ev_lam_q1, ev_lam_k1, ev_lam_q2, ev_lam_k2,
              ev_subln, od_w_in, od_q_norm, od_w_uq, od_kv_norm, od_w_ukv, od_w_out):
    bsz, seq, d = x.shape
    L = N_META + seq
    L_pad = -(-L // Q_BLOCK) * Q_BLOCK
    h = jnp.concatenate([
        jnp.broadcast_to(meta.astype(x.dtype)[None], (bsz, N_META, d)),
        x,
        jnp.zeros((bsz, L_pad - L, d), x.dtype),
    ], axis=1)
    pos = jnp.arange(L_pad)
    cid = chunk_id(pos)

    for i in range(DEPTH):
        j = i // 2
        if i % 2 == 0:
            y = even_layer(h, pos, cid, i, ev_w_in[j], ev_w_out[j], ev_lam_q1[j], ev_lam_k1[j],
                           ev_lam_q2[j], ev_lam_k2[j], ev_subln[j])
        else:
            y = odd_layer(h, pos, cid, od_w_in[j], od_q_norm[j], od_w_uq[j], od_kv_norm[j],
                          od_w_ukv[j], od_w_out[j])
        h = layer_norm(ALPHA * h + y, ln_g[i], ln_b[i])

    return h[:, N_META:N_META + seq]
```

```python
import functools
import math

import numpy as np
import jax
import jax.numpy as jnp
from jax import lax
from jax.experimental import pallas as pl
from jax.experimental.pallas import tpu as pltpu

D_MODEL = 1024
DEPTH = 4
CHUNK = 64
N_META = 16

A_HEADS = 4
A_DH = 64
A_DV = 128
A_QK = 512
A_W = 512
B_HEADS = 8
B_DH = 64
B_W = 512
EVEN_IN = 4096

C_HEADS = 8
C_NOPE = 128
C_ROPE = 64
C_DV = 128
C_QK = 192
Q_LORA = 512
KV_LORA = 256
C_W = 1024
ODD_IN_P = Q_LORA + KV_LORA + 2 * C_ROPE + C_W
C_SLOT = 256

ROPE_THETA = 10000.0
LN_EPS = 1e-5
RMS_EPS = 1e-6
ALPHA = (2.0 * DEPTH) ** 0.25

BQ = 256
BK = 256
MT = 256
NEG = -1e30
LANES = 128

VMEM_LIMIT = 48 * 1024 * 1024

bf16 = jnp.bfloat16
f32 = jnp.float32


def _cparams(sem):
    return pltpu.CompilerParams(dimension_semantics=sem, vmem_limit_bytes=VMEM_LIMIT)


def _mm_kernel(x_ref, w_ref, o_ref):
    o_ref[...] = jnp.dot(x_ref[...].astype(bf16), w_ref[...],
                         preferred_element_type=f32).astype(o_ref.dtype)


def _matmul(x, w, out_dtype, bm, bn):
    m, k = x.shape
    n = w.shape[1]
    return pl.pallas_call(
        _mm_kernel,
        out_shape=jax.ShapeDtypeStruct((m, n), out_dtype),
        grid=(m // bm, n // bn),
        in_specs=[pl.BlockSpec((bm, k), lambda i, j: (i, 0)),
                  pl.BlockSpec((k, bn), lambda i, j: (0, j))],
        out_specs=pl.BlockSpec((bm, bn), lambda i, j: (i, j)),
        compiler_params=_cparams(("parallel", "parallel")),
        name="proj",
    )(x, w)


def _post_kernel(*refs, n_in):
    y_refs = refs[:n_in]
    w_refs = refs[n_in:2 * n_in]
    h_ref, g_ref, b_ref, o_ref = refs[2 * n_in:]
    y = jnp.dot(y_refs[0][...], w_refs[0][...], preferred_element_type=f32)
    for a in range(1, n_in):
        y = y + jnp.dot(y_refs[a][...], w_refs[a][...], preferred_element_type=f32)
    t = ALPHA * h_ref[...] + y
    mu = jnp.mean(t, axis=1, keepdims=True)
    tc = t - mu
    var = jnp.mean(tc * tc, axis=1, keepdims=True)
    o_ref[...] = tc * lax.rsqrt(var + LN_EPS) * g_ref[...] + b_ref[...]


def _post(ys, ws, h, ln_g, ln_b, bm):
    m, d = h.shape
    n_in = len(ys)
    in_specs = ([pl.BlockSpec((bm, y.shape[1]), lambda i: (i, 0)) for y in ys]
                + [pl.BlockSpec(w.shape, lambda i: (0, 0)) for w in ws]
                + [pl.BlockSpec((bm, d), lambda i: (i, 0)),
                   pl.BlockSpec((1, d), lambda i: (0, 0)),
                   pl.BlockSpec((1, d), lambda i: (0, 0))])
    return pl.pallas_call(
        functools.partial(_post_kernel, n_in=n_in),
        out_shape=jax.ShapeDtypeStruct((m, d), f32),
        grid=(m // bm,),
        in_specs=in_specs,
        out_specs=pl.BlockSpec((bm, d), lambda i: (i, 0)),
        compiler_params=_cparams(("parallel",)),
        name="post",
    )(*ys, *ws, h, ln_g.reshape(1, d), ln_b.reshape(1, d))


def _tile_masks():
    row = lax.broadcasted_iota(jnp.int32, (BQ, BK), 0)
    col = lax.broadcasted_iota(jnp.int32, (BQ, BK), 1)
    return row, col


def _dot_nt(a, b):
    return lax.dot_general(a, b, (((1,), (1,)), ((), ())), preferred_element_type=f32)


def _softmax_update(s, v, m_sc, l_sc, acc_sc, idx):
    m_old = m_sc[idx]
    m_new = jnp.maximum(m_old, jnp.max(s, axis=1, keepdims=True))
    alpha = jnp.exp(m_old - m_new)
    p = jnp.exp(s - m_new)
    l_sc[idx] = alpha * l_sc[idx] + jnp.sum(p, axis=1, keepdims=True)
    acc_sc[idx] = alpha * acc_sc[idx] + jnp.dot(p.astype(bf16), v, preferred_element_type=f32)
    m_sc[idx] = m_new


def _silu(g):
    return g * (1.0 / (1.0 + jnp.exp(-g)))


def _attn_a_kernel(slope_ref, lq1_ref, lk1_ref, lq2_ref, lk2_ref, sub_ref,
                   q_ref, k_ref, v_ref, g_ref, o_ref, m_sc, l_sc, acc_sc, *, lam_init):
    h = pl.program_id(1)
    qi = pl.program_id(2)
    slope = slope_ref[h]
    m_sc[...] = jnp.full(m_sc.shape, NEG, f32)
    l_sc[...] = jnp.zeros(l_sc.shape, f32)
    acc_sc[...] = jnp.zeros(acc_sc.shape, f32)

    q = q_ref[0] * jnp.asarray(A_DH ** -0.5, bf16)
    lane = lax.broadcasted_iota(jnp.int32, q.shape, 1)
    zero = jnp.zeros_like(q)
    qm = (jnp.where(lane < A_DH, q, zero), jnp.where(lane >= A_DH, q, zero))
    row, col = _tile_masks()
    d0 = row - col
    q0 = qi * BQ

    def tile(k0, mode):
        k = k_ref[0, pl.ds(k0, BK), :]
        v = v_ref[0, pl.ds(k0, BK), :]
        bias = slope * jnp.abs(d0 + (q0 - k0)).astype(f32)
        for m in range(2):
            s = _dot_nt(qm[m], k) - bias
            if mode == "meta":
                s = jnp.where(col >= BK - N_META, s, NEG)
            elif mode == "diag":
                s = jnp.where((col >> 6) <= (row >> 6), s, NEG)
            _softmax_update(s, v, m_sc, l_sc, acc_sc, m)

    tile(0, "meta")

    @pl.when(qi >= 1)
    def _():
        def body(kt, carry):
            tile(pl.multiple_of(kt * BK, BK), "full")
            return carry
        lax.fori_loop(1, qi, body, 0)
        tile(pl.multiple_of(qi * BK, BK), "diag")

    lam = (jnp.exp(jnp.sum(lq1_ref[...] * lk1_ref[...], axis=1, keepdims=True))
           - jnp.exp(jnp.sum(lq2_ref[...] * lk2_ref[...], axis=1, keepdims=True))
           + lam_init)
    o = acc_sc[0] / l_sc[0] - lam * (acc_sc[1] / l_sc[1])
    ms = jnp.mean(o * o, axis=1, keepdims=True)
    o = o * lax.rsqrt(ms + RMS_EPS) * sub_ref[...] * (1.0 - lam_init)
    o_ref[0] = (o * _silu(g_ref[0].astype(f32))).astype(o_ref.dtype)


def _attn_a(pe, slopes, lq1, lk1, lq2, lk2, subln, lam_init):
    bsz, lp, _ = pe.shape
    nq = lp // BQ
    vec = lambda n: pl.BlockSpec((1, n), lambda b, h, i: (0, 0))
    return pl.pallas_call(
        functools.partial(_attn_a_kernel, lam_init=lam_init),
        out_shape=jax.ShapeDtypeStruct((bsz, lp, A_W), bf16),
        grid=(bsz, A_HEADS, nq),
        in_specs=[pl.BlockSpec(memory_space=pltpu.SMEM),
                  vec(A_DH), vec(A_DH), vec(A_DH), vec(A_DH), vec(A_DV),
                  pl.BlockSpec((1, BQ, LANES), lambda b, h, i: (b, i, h)),
                  pl.BlockSpec((1, lp, LANES), lambda b, h, i: (b, 0, 4 + h)),
                  pl.BlockSpec((1, lp, LANES), lambda b, h, i: (b, 0, 8 + h)),
                  pl.BlockSpec((1, BQ, LANES), lambda b, h, i: (b, i, 12 + h))],
        out_specs=pl.BlockSpec((1, BQ, LANES), lambda b, h, i: (b, i, h)),
        scratch_shapes=[pltpu.VMEM((2, BQ, 1), f32), pltpu.VMEM((2, BQ, 1), f32),
                        pltpu.VMEM((2, BQ, A_DV), f32)],
        compiler_params=_cparams(("parallel", "parallel", "arbitrary")),
        name="attn_a",
    )(slopes, lq1.reshape(1, -1), lk1.reshape(1, -1), lq2.reshape(1, -1), lk2.reshape(1, -1),
      subln.reshape(1, -1), pe, pe, pe, pe)


def _attn_b_kernel(q_ref, k_ref, v_ref, g_ref, o_ref, c_sc, acc_sc):
    qi = pl.program_id(2)
    c_sc[...] = jnp.zeros(c_sc.shape, f32)
    acc_sc[...] = jnp.zeros(acc_sc.shape, f32)

    q = q_ref[0] * jnp.asarray(B_DH ** -0.5, bf16)
    lane = lax.broadcasted_iota(jnp.int32, q.shape, 1)
    zero = jnp.zeros_like(q)
    qm = (jnp.where(lane < B_DH, q, zero), jnp.where(lane >= B_DH, q, zero))
    row, col = _tile_masks()
    upper = jnp.where(row > col, 1.0, 0.0).astype(bf16)

    def tile(k0, mode):
        k = k_ref[0, pl.ds(k0, BK), :]
        v = v_ref[0, pl.ds(k0, BK), :]
        if mode == "meta":
            mask = col >= BK - N_META
        elif mode == "diag":
            mask = col < row
        elif mode == "metadiag":
            mask = (col >= BK - N_META) & (col < row)
        else:
            mask = None
        for hh in range(2):
            z = _dot_nt(qm[hh], k)
            sp = jnp.maximum(z, 0.0) + jnp.log(1.0 + jnp.exp(-jnp.abs(z)))
            ls = -sp
            if mask is not None:
                ls = jnp.where(mask, ls, 0.0)
            hi = ls.astype(bf16)
            lo = (ls - hi.astype(f32)).astype(bf16)
            cum = (jnp.dot(hi, upper, preferred_element_type=f32)
                   + jnp.dot(lo, upper, preferred_element_type=f32))
            a = jnp.exp((z - sp) + (cum + c_sc[hh]))
            if mask is not None:
                a = jnp.where(mask, a, 0.0)
            acc_sc[hh] = acc_sc[hh] + jnp.dot(a.astype(bf16), v, preferred_element_type=f32)
            c_sc[hh] = c_sc[hh] + jnp.sum(ls, axis=1, keepdims=True)

    @pl.when(qi == 0)
    def _():
        tile(0, "metadiag")

    @pl.when(qi >= 1)
    def _():
        tile(pl.multiple_of(qi * BK, BK), "diag")

        def body(t, carry):
            tile(pl.multiple_of((qi - 1 - t) * BK, BK), "full")
            return carry
        lax.fori_loop(0, qi - 1, body, 0)
        tile(0, "meta")

    lane_o = lax.broadcasted_iota(jnp.int32, (BQ, LANES), 1)
    o = jnp.where(lane_o < B_DH, acc_sc[0], acc_sc[1])
    o_ref[0] = (o * _silu(g_ref[0].astype(f32))).astype(o_ref.dtype)


def _attn_b(pe):
    bsz, lp, _ = pe.shape
    nq = lp // BQ
    return pl.pallas_call(
        _attn_b_kernel,
        out_shape=jax.ShapeDtypeStruct((bsz, lp, B_W), bf16),
        grid=(bsz, B_HEADS // 2, nq),
        in_specs=[pl.BlockSpec((1, BQ, LANES), lambda b, p, i: (b, i, 16 + p)),
                  pl.BlockSpec((1, lp, LANES), lambda b, p, i: (b, 0, 20 + p)),
                  pl.BlockSpec((1, lp, LANES), lambda b, p, i: (b, 0, 24 + p)),
                  pl.BlockSpec((1, BQ, LANES), lambda b, p, i: (b, i, 28 + p))],
        out_specs=pl.BlockSpec((1, BQ, LANES), lambda b, p, i: (b, i, p)),
        scratch_shapes=[pltpu.VMEM((2, BQ, 1), f32), pltpu.VMEM((2, BQ, LANES), f32)],
        compiler_params=_cparams(("parallel", "parallel", "arbitrary")),
        name="attn_b",
    )(pe, pe, pe, pe)


def _attn_c_kernel(q_ref, k_ref, v_ref, g_ref, o_ref, m_sc, l_sc, acc_sc):
    qi = pl.program_id(2)
    m_sc[...] = jnp.full(m_sc.shape, NEG, f32)
    l_sc[...] = jnp.zeros(l_sc.shape, f32)
    acc_sc[...] = jnp.zeros(acc_sc.shape, f32)
    q = q_ref[0]
    row, col = _tile_masks()

    def tile(k0, mode):
        k = k_ref[0, pl.ds(k0, BK), :]
        v = v_ref[0, pl.ds(k0, BK), :]
        s = _dot_nt(q, k)
        if mode == "meta":
            s = jnp.where(col >= BK - N_META, s, NEG)
        elif mode == "diag":
            s = jnp.where((col >> 6) <= (row >> 6), s, NEG)
        _softmax_update(s, v, m_sc, l_sc, acc_sc, 0)

    tile(0, "meta")

    @pl.when(qi >= 1)
    def _():
        def body(kt, carry):
            tile(pl.multiple_of(kt * BK, BK), "full")
            return carry
        lax.fori_loop(1, qi, body, 0)
        tile(pl.multiple_of(qi * BK, BK), "diag")

    o = acc_sc[0] / l_sc[0]
    o_ref[0] = (o * _silu(g_ref[0].astype(f32))).astype(o_ref.dtype)


def _attn_c(qp, kp, vv, po):
    bsz, lp, _ = qp.shape
    nq = lp // BQ
    g_blk0 = (Q_LORA + KV_LORA + 2 * C_ROPE) // LANES
    return pl.pallas_call(
        _attn_c_kernel,
        out_shape=jax.ShapeDtypeStruct((bsz, lp, C_W), bf16),
        grid=(bsz, C_HEADS, nq),
        in_specs=[pl.BlockSpec((1, BQ, C_SLOT), lambda b, h, i: (b, i, h)),
                  pl.BlockSpec((1, lp, C_SLOT), lambda b, h, i: (b, 0, h)),
                  pl.BlockSpec((1, lp, C_DV), lambda b, h, i: (b, 0, h)),
                  pl.BlockSpec((1, BQ, LANES), lambda b, h, i: (b, i, g_blk0 + h))],
        out_specs=pl.BlockSpec((1, BQ, C_DV), lambda b, h, i: (b, i, h)),
        scratch_shapes=[pltpu.VMEM((1, BQ, 1), f32), pltpu.VMEM((1, BQ, 1), f32),
                        pltpu.VMEM((1, BQ, C_DV), f32)],
        compiler_params=_cparams(("parallel", "parallel", "arbitrary")),
        name="attn_c",
    )(qp, kp, vv, po)


def _rms(x, g):
    return x * lax.rsqrt(jnp.mean(x * x, axis=1, keepdims=True) + RMS_EPS) * g


def _odd_mid_kernel(cq_ref, ckv_ref, kpe_ref, cos_ref, sin_ref, qn_ref, wq_ref, kvn_ref, wkv_ref,
                    qp_ref, kp_ref, v_ref):
    cos2 = cos_ref[...]
    sin2 = sin_ref[...]

    def rope(xx):
        return xx * cos2 + pltpu.roll(xx, C_ROPE, 1) * sin2

    scale = C_QK ** -0.5
    cq = _rms(cq_ref[...].astype(f32), qn_ref[...]).astype(bf16)
    qf = jnp.dot(cq, wq_ref[...], preferred_element_type=f32)
    for hh in range(C_HEADS):
        base = hh * C_SLOT
        qp_ref[:, base:base + C_NOPE] = (qf[:, base:base + C_NOPE] * scale).astype(bf16)
        qp_ref[:, base + C_NOPE:base + C_SLOT] = (rope(qf[:, base + C_NOPE:base + C_SLOT]) * scale).astype(bf16)

    ckv = _rms(ckv_ref[...].astype(f32), kvn_ref[...]).astype(bf16)
    kvf = jnp.dot(ckv, wkv_ref[...], preferred_element_type=f32)
    kpe = rope(kpe_ref[...].astype(f32)).astype(bf16)
    for hh in range(C_HEADS):
        base = hh * C_SLOT
        kp_ref[:, base:base + C_NOPE] = kvf[:, base:base + C_NOPE].astype(bf16)
        kp_ref[:, base + C_NOPE:base + C_SLOT] = kpe
        v_ref[:, hh * C_DV:(hh + 1) * C_DV] = kvf[:, base + C_NOPE:base + C_SLOT].astype(bf16)


def _odd_mid(po, cos2, sin2, q_norm, wq, kv_norm, wkv, lp, bm):
    m = po.shape[0]
    nb = lp // bm
    kv_blk = Q_LORA // KV_LORA
    kpe_blk = (Q_LORA + KV_LORA) // LANES
    full = lambda a: pl.BlockSpec(a.shape, lambda i: (0, 0))
    qn = q_norm.reshape(1, -1)
    kvn = kv_norm.reshape(1, -1)
    return pl.pallas_call(
        _odd_mid_kernel,
        out_shape=(jax.ShapeDtypeStruct((m, C_HEADS * C_SLOT), bf16),
                   jax.ShapeDtypeStruct((m, C_HEADS * C_SLOT), bf16),
                   jax.ShapeDtypeStruct((m, C_W), bf16)),
        grid=(m // bm,),
        in_specs=[pl.BlockSpec((bm, Q_LORA), lambda i: (i, 0)),
                  pl.BlockSpec((bm, KV_LORA), lambda i: (i, kv_blk)),
                  pl.BlockSpec((bm, LANES), lambda i: (i, kpe_blk)),
                  pl.BlockSpec((bm, LANES), lambda i: (i % nb, 0)),
                  pl.BlockSpec((bm, LANES), lambda i: (i % nb, 0)),
                  full(qn), full(wq), full(kvn), full(wkv)],
        out_specs=(pl.BlockSpec((bm, C_HEADS * C_SLOT), lambda i: (i, 0)),
                   pl.BlockSpec((bm, C_HEADS * C_SLOT), lambda i: (i, 0)),
                   pl.BlockSpec((bm, C_W), lambda i: (i, 0))),
        compiler_params=_cparams(("parallel",)),
        name="odd_mid",
    )(po, po, po, cos2, sin2, qn, wq, kvn, wkv)


def _rot_cols(w):
    half = w.shape[-1] // 2
    return jnp.concatenate([-w[..., half:], w[..., :half]], axis=-1)


def _prep_odd_w_in(w):
    o1 = Q_LORA + KV_LORA
    kpe = w[:, o1:o1 + C_ROPE]
    return jnp.concatenate([w[:, :o1], kpe, _rot_cols(kpe), w[:, o1 + C_ROPE:]], axis=1).astype(bf16)


def _prep_w_uq(w):
    w3 = w.reshape(Q_LORA, C_HEADS, C_QK)
    rope_w = w3[:, :, C_NOPE:]
    return jnp.concatenate([w3[:, :, :C_NOPE], rope_w, _rot_cols(rope_w)], axis=2).reshape(
        Q_LORA, C_HEADS * C_SLOT).astype(bf16)


def _rope_tables(lp):
    half = C_ROPE // 2
    u = jnp.arange(lp) - MT
    pos = jnp.where(u >= -N_META, u + N_META, 0)
    inv_freq = ROPE_THETA ** (-jnp.arange(half, dtype=f32) / half)
    ang = pos.astype(f32)[:, None] * inv_freq[None, :]
    zeros = jnp.zeros((lp, LANES - C_ROPE), f32)
    cos = jnp.cos(ang)
    sin = jnp.sin(ang)
    return (jnp.concatenate([cos, cos, zeros], axis=1), jnp.concatenate([sin, sin, zeros], axis=1))


def kernel(x, meta, ln_g, ln_b, ev_w_in, ev_w_out, ev_lam_q1, ev_lam_k1, ev_lam_q2, ev_lam_k2,
           ev_subln, od_w_in, od_q_norm, od_w_uq, od_kv_norm, od_w_ukv, od_w_out):
    bsz, seq, d = x.shape
    assert d == D_MODEL and seq % BQ == 0
    lp = MT + seq
    rows = bsz * lp
    bm = 512 if rows % 512 == 0 else 256

    h = jnp.concatenate([
        jnp.zeros((bsz, MT - N_META, d), x.dtype),
        jnp.broadcast_to(meta.astype(x.dtype)[None], (bsz, N_META, d)),
        x], axis=1).reshape(rows, d)

    slopes = jnp.asarray(2.0 ** (-8.0 * np.arange(1, A_HEADS + 1) / A_HEADS), dtype=f32)
    cos2, sin2 = _rope_tables(lp)

    for i in range(DEPTH):
        j = i // 2
        if i % 2 == 0:
            lam_init = 0.8 - 0.6 * math.exp(-0.3 * i)
            pe = _matmul(h, ev_w_in[j].astype(bf16), bf16, bm, 1024).reshape(bsz, lp, EVEN_IN)
            ya = _attn_a(pe, slopes, ev_lam_q1[j], ev_lam_k1[j], ev_lam_q2[j], ev_lam_k2[j],
                         ev_subln[j], lam_init)
            ys = _attn_b(pe)
            w_out = ev_w_out[j].astype(bf16)
            h = _post([ya.reshape(rows, A_W), ys.reshape(rows, B_W)], [w_out[:A_W], w_out[A_W:]],
                      h, ln_g[i], ln_b[i], bm)
        else:
            po = _matmul(h, _prep_odd_w_in(od_w_in[j]), f32, bm, ODD_IN_P // 3)
            qp, kp, vv = _odd_mid(po, cos2, sin2, od_q_norm[j], _prep_w_uq(od_w_uq[j]),
                                  od_kv_norm[j], od_w_ukv[j].astype(bf16), lp, 256)
            o = _attn_c(qp.reshape(bsz, lp, -1), kp.reshape(bsz, lp, -1), vv.reshape(bsz, lp, -1),
                        po.reshape(bsz, lp, -1))
            h = _post([o.reshape(rows, C_W)], [od_w_out[j].astype(bf16)], h, ln_g[i], ln_b[i], bm)

    return h.reshape(bsz, lp, d)[:, MT:, :]
```

```python
import functools
import math

import numpy as np
import jax
import jax.numpy as jnp
from jax import lax
from jax.experimental import pallas as pl
from jax.experimental.pallas import tpu as pltpu

D_MODEL = 1024
DEPTH = 4
CHUNK = 64
N_META = 16

A_HEADS = 4
A_DH = 64
A_DV = 128
A_QK = 512
A_W = 512
B_HEADS = 8
B_DH = 64
B_W = 512
EVEN_IN = 4096

C_HEADS = 8
C_NOPE = 128
C_ROPE = 64
C_DV = 128
C_QK = 192
Q_LORA = 512
KV_LORA = 256
C_W = 1024
ODD_G0 = 0
ODD_CQ0 = C_W
ODD_CKV0 = ODD_CQ0 + Q_LORA
ODD_KPE0 = ODD_CKV0 + KV_LORA
ODD_IN_P = ODD_KPE0 + 2 * C_ROPE
C_SLOT = 256
C_MASK_LANE = C_ROPE

ROPE_THETA = 10000.0
LN_EPS = 1e-5
RMS_EPS = 1e-6
ALPHA = (2.0 * DEPTH) ** 0.25
LOG2E = math.log2(math.e)

LANES = 128
MT = 512
BQ = 512
BK = 512
MS = 128
BQB = 256
NEG = -1e30
SB_DEAD = -104.0

VMEM_LIMIT = 52 * 1024 * 1024

bf16 = jnp.bfloat16
f32 = jnp.float32


def _cparams(sem):
    return pltpu.CompilerParams(dimension_semantics=sem, vmem_limit_bytes=VMEM_LIMIT)


def _dot_nt(a, b):
    return lax.dot_general(a, b, (((1,), (1,)), ((), ())), preferred_element_type=f32)


def _mm_kernel(x_ref, w_ref, o_ref):
    o_ref[...] = jnp.dot(x_ref[...].astype(bf16), w_ref[...],
                         preferred_element_type=f32).astype(o_ref.dtype)


def _matmul(x, w, out_dtype, bm, bn):
    m, k = x.shape
    n = w.shape[1]
    return pl.pallas_call(
        _mm_kernel,
        out_shape=jax.ShapeDtypeStruct((m, n), out_dtype),
        grid=(m // bm, n // bn),
        in_specs=[pl.BlockSpec((bm, k), lambda i, j: (i, 0)),
                  pl.BlockSpec((k, bn), lambda i, j: (0, j))],
        out_specs=pl.BlockSpec((bm, bn), lambda i, j: (i, j)),
        compiler_params=_cparams(("parallel", "parallel")),
        name="proj",
    )(x, w)


EVEN_MAIN = EVEN_IN - A_W


def _even_proj_kernel(x_ref, w_ref, wvt_ref, o_ref, vt_ref):
    x = x_ref[...].astype(bf16)
    o_ref[...] = jnp.dot(x, w_ref[...], preferred_element_type=f32).astype(o_ref.dtype)

    @pl.when(pl.program_id(1) == 0)
    def _():
        vt_ref[0] = _dot_nt(wvt_ref[...], x).astype(vt_ref.dtype)


def _even_proj(x, w, wvt, bn):
    m, k = x.shape
    bm = BK
    return pl.pallas_call(
        _even_proj_kernel,
        out_shape=(jax.ShapeDtypeStruct((m, EVEN_MAIN), bf16),
                   jax.ShapeDtypeStruct((m // bm, A_W, bm), bf16)),
        grid=(m // bm, EVEN_MAIN // bn),
        in_specs=[pl.BlockSpec((bm, k), lambda i, j: (i, 0)),
                  pl.BlockSpec((k, bn), lambda i, j: (0, j)),
                  pl.BlockSpec((A_W, k), lambda i, j: (0, 0))],
        out_specs=(pl.BlockSpec((bm, bn), lambda i, j: (i, j)),
                   pl.BlockSpec((1, A_W, bm), lambda i, j: (i, 0, 0))),
        compiler_params=_cparams(("parallel", "arbitrary")),
        name="even_proj",
    )(x, w, wvt)


def _post_kernel(*refs, n_in):
    y_refs = refs[:n_in]
    w_refs = refs[n_in:2 * n_in]
    h_ref, g_ref, b_ref, o_ref = refs[2 * n_in:]
    y = jnp.dot(y_refs[0][...], w_refs[0][...], preferred_element_type=f32)
    for a in range(1, n_in):
        y = y + jnp.dot(y_refs[a][...], w_refs[a][...], preferred_element_type=f32)
    t = ALPHA * h_ref[...] + y
    mu = jnp.mean(t, axis=1, keepdims=True)
    tc = t - mu
    var = jnp.mean(tc * tc, axis=1, keepdims=True)
    o_ref[...] = tc * lax.rsqrt(var + LN_EPS) * g_ref[...] + b_ref[...]


def _post(ys, ws, h, ln_g, ln_b, bm):
    m, d = h.shape
    n_in = len(ys)
    in_specs = ([pl.BlockSpec((bm, y.shape[1]), lambda i: (i, 0)) for y in ys]
                + [pl.BlockSpec(w.shape, lambda i: (0, 0)) for w in ws]
                + [pl.BlockSpec((bm, d), lambda i: (i, 0)),
                   pl.BlockSpec((1, d), lambda i: (0, 0)),
                   pl.BlockSpec((1, d), lambda i: (0, 0))])
    return pl.pallas_call(
        functools.partial(_post_kernel, n_in=n_in),
        out_shape=jax.ShapeDtypeStruct((m, d), f32),
        grid=(m // bm,),
        in_specs=in_specs,
        out_specs=pl.BlockSpec((bm, d), lambda i: (i, 0)),
        compiler_params=_cparams(("parallel",)),
        name="post",
    )(*ys, *ws, h, ln_g.reshape(1, d), ln_b.reshape(1, d))


def _softmax_block(pieces, m_sc, l_sc, acc_sc, idx, first):
    row_max = jnp.max(pieces[0][0], axis=1, keepdims=True)
    for s, _ in pieces[1:]:
        row_max = jnp.maximum(row_max, jnp.max(s, axis=1, keepdims=True))
    m_new = row_max if first else jnp.maximum(m_sc[idx], row_max)
    row_sum = None
    pv = None
    for s, v in pieces:
        p = jnp.exp2(s - m_new)
        ps = jnp.sum(p, axis=1, keepdims=True)
        pp = jnp.dot(p.astype(bf16), v, preferred_element_type=f32)
        row_sum = ps if row_sum is None else row_sum + ps
        pv = pp if pv is None else pv + pp
    if first:
        l_sc[idx] = row_sum
        acc_sc[idx] = pv
    else:
        alpha = jnp.exp2(m_sc[idx] - m_new)
        l_sc[idx] = alpha * l_sc[idx] + row_sum
        acc_sc[idx] = alpha * acc_sc[idx] + pv
    m_sc[idx] = m_new


def _softmax_block_t(pieces, m_sc, l_sc, acc_sc, idx, first):
    col_max = jnp.max(pieces[0][0], axis=0, keepdims=True)
    for s, _ in pieces[1:]:
        col_max = jnp.maximum(col_max, jnp.max(s, axis=0, keepdims=True))
    m_new = col_max if first else jnp.maximum(m_sc[idx], col_max)
    col_sum = None
    pv = None
    for s, vt in pieces:
        p = jnp.exp2(s - m_new)
        ps = jnp.sum(p, axis=0, keepdims=True)
        pp = jnp.dot(vt, p.astype(bf16), preferred_element_type=f32)
        col_sum = ps if col_sum is None else col_sum + ps
        pv = pp if pv is None else pv + pp
    if first:
        l_sc[idx] = col_sum
        acc_sc[idx] = pv
    else:
        alpha = jnp.exp2(m_sc[idx] - m_new)
        l_sc[idx] = alpha * l_sc[idx] + col_sum
        acc_sc[idx] = alpha * acc_sc[idx] + pv
    m_sc[idx] = m_new


def _run_chains(n, scores, consume):
    s_next = scores(0)
    for c in range(n):
        s_cur = s_next
        if c + 1 < n:
            s_next = scores(c + 1)
        consume(c, s_cur)


def _silu(g):
    return g * (1.0 / (1.0 + jnp.exp(-g)))


def _chunk_visible(row, col):
    return (col >> 6) <= (row >> 6)


A_GROUP = 2


def _attn_a_kernel(slope_ref, lq1_ref, lk1_ref, lq2_ref, lk2_ref, sub_ref,
                   q_ref, k_ref, kb_ref, vt_ref, g_ref, o_ref,
                   m_sc, l_sc, acc_sc, ediag_sc, emeta_sc, *, lam_init):
    hp = pl.program_id(1)
    qi = pl.program_id(2)

    @pl.when(qi == 0)
    def _():
        for hh in range(A_GROUP):
            c2 = 2.0 * LOG2E * slope_ref[hp * A_GROUP + hh]
            krow = lax.broadcasted_iota(jnp.int32, (BK, BQ), 0)
            qcol = lax.broadcasted_iota(jnp.int32, (BK, BQ), 1)
            ahead = jnp.maximum(krow - qcol, 0).astype(f32) * c2
            ediag_sc[hh] = jnp.where(_chunk_visible(qcol, krow), ahead, -NEG)
            krow = lax.broadcasted_iota(jnp.int32, (MS, BQ), 0)
            qcol = lax.broadcasted_iota(jnp.int32, (MS, BQ), 1)
            emeta_sc[hh] = jnp.maximum((krow - (MS - N_META)) - (qcol - (BQ - N_META)), 0).astype(f32) * c2

    q = q_ref[0]
    lane = lax.broadcasted_iota(jnp.int32, (BQ, LANES), 1)
    ones = jnp.where(lane < 3, 1.0, 0.0).astype(bf16)
    chains = [(hh, m) for hh in range(A_GROUP) for m in range(2)]
    q_aug = []
    for hh, m in chains:
        qh = q[:, hh * LANES:(hh + 1) * LANES]
        keep = (lane < A_DH) if m == 0 else (lane >= A_DH)
        q_aug.append(jnp.concatenate([jnp.where(keep, qh, jnp.zeros_like(qh)), ones], axis=1))

    def keys(hh, k0, n):
        return jnp.concatenate([k_ref[0, pl.ds(k0, n), hh * LANES:(hh + 1) * LANES],
                                kb_ref[hh, pl.ds(k0, n), :]], axis=1)

    def vals_t(hh, kt):
        return vt_ref[kt, hh * A_DV:(hh + 1) * A_DV, :]

    def meta_scores(ci):
        return _dot_nt(keys(chains[ci][0], MT - MS, MS), q_aug[ci])

    def meta_vals_t(hh):
        return vt_ref[MT // BK - 1, hh * A_DV:(hh + 1) * A_DV, BK - MS:]

    n_ch = len(chains)

    @pl.when(qi == 0)
    def _():
        _run_chains(n_ch, meta_scores,
                    lambda ci, s: _softmax_block_t([(s - emeta_sc[chains[ci][0]], meta_vals_t(chains[ci][0]))],
                                                   m_sc, l_sc, acc_sc, ci, True))

    @pl.when(qi >= 1)
    def _():
        k0 = pl.multiple_of(qi * BK, BK)

        def first_scores(ci):
            return _dot_nt(keys(chains[ci][0], k0, BK), q_aug[ci]), meta_scores(ci)

        def first_consume(ci, s):
            hh = chains[ci][0]
            _softmax_block_t([(s[0] - ediag_sc[hh], vals_t(hh, qi)), (s[1], meta_vals_t(hh))],
                             m_sc, l_sc, acc_sc, ci, True)

        _run_chains(n_ch, first_scores, first_consume)

        def body(kt, carry):
            kk = pl.multiple_of(kt * BK, BK)
            _run_chains(n_ch, lambda ci: _dot_nt(keys(chains[ci][0], kk, BK), q_aug[ci]),
                        lambda ci, s: _softmax_block_t([(s, vals_t(chains[ci][0], kt))],
                                                       m_sc, l_sc, acc_sc, ci, False))
            return carry
        lax.fori_loop(1, qi, body, 0)

    lam = (jnp.exp(jnp.sum(lq1_ref[...] * lk1_ref[...], axis=1, keepdims=True))
           - jnp.exp(jnp.sum(lq2_ref[...] * lk2_ref[...], axis=1, keepdims=True))
           + lam_init)
    g = g_ref[0].astype(f32)
    for hh in range(A_GROUP):
        o = (acc_sc[2 * hh] / l_sc[2 * hh] - lam * (acc_sc[2 * hh + 1] / l_sc[2 * hh + 1])).T
        ms = jnp.mean(o * o, axis=1, keepdims=True)
        o = o * lax.rsqrt(ms + RMS_EPS) * sub_ref[...] * (1.0 - lam_init)
        cs = slice(hh * A_DV, (hh + 1) * A_DV)
        o_ref[0, :, cs] = (o * _silu(g[:, cs])).astype(o_ref.dtype)


def _attn_a(pe, vt, kbias, slopes, lq1, lk1, lq2, lk2, subln, lam_init):
    bsz, lp, _ = pe.shape
    nq = lp // BQ
    nk = lp // BK
    w = A_GROUP * LANES
    n_grp = A_HEADS // A_GROUP
    vec = lambda n: pl.BlockSpec((1, n), lambda b, h, i: (0, 0))
    return pl.pallas_call(
        functools.partial(_attn_a_kernel, lam_init=lam_init),
        out_shape=jax.ShapeDtypeStruct((bsz, lp, A_W), bf16),
        grid=(bsz, n_grp, nq),
        in_specs=[pl.BlockSpec(memory_space=pltpu.SMEM),
                  vec(A_DH), vec(A_DH), vec(A_DH), vec(A_DH), vec(A_DV),
                  pl.BlockSpec((1, BQ, w), lambda b, h, i: (b, i, h)),
                  pl.BlockSpec((1, lp, w), lambda b, h, i: (b, 0, n_grp + h)),
                  pl.BlockSpec((A_GROUP, lp, LANES), lambda b, h, i: (h, 0, 0)),
                  pl.BlockSpec((nk, A_GROUP * A_DV, BK), lambda b, h, i: (b, h, 0)),
                  pl.BlockSpec((1, BQ, w), lambda b, h, i: (b, i, 2 * n_grp + h))],
        out_specs=pl.BlockSpec((1, BQ, w), lambda b, h, i: (b, i, h)),
        scratch_shapes=[pltpu.VMEM((2 * A_GROUP, 1, BQ), f32), pltpu.VMEM((2 * A_GROUP, 1, BQ), f32),
                        pltpu.VMEM((2 * A_GROUP, A_DV, BQ), f32),
                        pltpu.VMEM((A_GROUP, BK, BQ), f32), pltpu.VMEM((A_GROUP, MS, BQ), f32)],
        compiler_params=_cparams(("parallel", "parallel", "arbitrary")),
        name="attn_a",
    )(slopes, lq1.reshape(1, -1), lk1.reshape(1, -1), lq2.reshape(1, -1), lk2.reshape(1, -1),
      subln.reshape(1, -1), pe, pe, kbias, vt, pe)


def _alibi_key_columns(slopes, lp):
    u = jnp.arange(lp) - MT
    pos = (u + N_META).astype(f32)
    b = (slopes * LOG2E)[:, None] * pos[None, :]

    def top8(v):
        return lax.bitcast_convert_type(lax.bitcast_convert_type(v, jnp.uint32) & jnp.uint32(0xFFFF0000), f32)

    hi32 = top8(b)
    r1 = b - hi32
    mid32 = top8(r1)
    hi, mid, lo = hi32.astype(bf16), mid32.astype(bf16), (r1 - mid32).astype(bf16)
    valid = (u >= -N_META)[None, :]
    hi = jnp.where(valid, hi, jnp.asarray(NEG, bf16))
    mid = jnp.where(valid, mid, jnp.zeros_like(mid))
    lo = jnp.where(valid, lo, jnp.zeros_like(lo))
    cols = jnp.stack([hi, mid, lo], axis=-1)
    return jnp.concatenate([cols, jnp.zeros((slopes.shape[0], lp, LANES - 3), bf16)], axis=-1)


B_GROUP = 4


def _attn_b_kernel(q_ref, k_ref, v_ref, g_ref, o_ref, c_sc, acc_sc):
    qi = pl.program_id(2)
    meta_tile = MT // BQB - 1

    @pl.when(qi < meta_tile)
    def _():
        o_ref[0] = jnp.zeros(o_ref.shape[1:], o_ref.dtype)

    @pl.when(qi >= meta_tile)
    def _():
        c_sc[...] = jnp.zeros(c_sc.shape, f32)
        acc_sc[...] = jnp.zeros(acc_sc.shape, f32)
        q = q_ref[0]
        lane = lax.broadcasted_iota(jnp.int32, (BQB, LANES), 1)
        qm = []
        for hh in range(B_GROUP):
            qp = q[:, (hh // 2) * LANES:(hh // 2 + 1) * LANES]
            keep = (lane < B_DH) if hh % 2 == 0 else (lane >= B_DH)
            qm.append(jnp.where(keep, qp, jnp.zeros_like(qp)))
        row = lax.broadcasted_iota(jnp.int32, (BQB, BQB), 0)
        col = lax.broadcasted_iota(jnp.int32, (BQB, BQB), 1)
        after = jnp.where(row > col, 1.0, 0.0).astype(bf16)

        def tile(k0, mode):
            if mode == "meta":
                mask = col >= BQB - N_META
            elif mode == "diag":
                mask = col < row
            elif mode == "metadiag":
                mask = (col >= BQB - N_META) & (col < row)
            else:
                mask = None
            for hh in range(B_GROUP):
                cs = slice((hh // 2) * LANES, (hh // 2 + 1) * LANES)
                k = k_ref[0, pl.ds(k0, BQB), cs]
                v = v_ref[0, pl.ds(k0, BQB), cs]
                z = _dot_nt(qm[hh], k)
                sp = jnp.maximum(z, 0.0) + jnp.log(1.0 + jnp.exp(-jnp.abs(z)))
                ls = -sp
                if mask is not None:
                    ls = jnp.where(mask, ls, 0.0)
                hi = ls.astype(bf16)
                lo = (ls - hi.astype(f32)).astype(bf16)
                cum = (jnp.dot(hi, after, preferred_element_type=f32)
                       + jnp.dot(lo, after, preferred_element_type=f32))
                a = jnp.exp((z - sp) + (cum + c_sc[hh]))
                if mask is not None:
                    a = jnp.where(mask, a, 0.0)
                acc_sc[hh] = acc_sc[hh] + jnp.dot(a.astype(bf16), v, preferred_element_type=f32)
                c_sc[hh] = c_sc[hh] + jnp.sum(ls, axis=1, keepdims=True)

        def alive():
            return jnp.max(c_sc[...])

        @pl.when(qi == meta_tile)
        def _():
            tile(meta_tile * BQB, "metadiag")

        @pl.when(qi > meta_tile)
        def _():
            tile(pl.multiple_of(qi * BQB, BQB), "diag")

            def cond(carry):
                kt, top = carry
                return jnp.logical_and(kt > meta_tile, top > SB_DEAD)

            def body(carry):
                kt, _ = carry
                tile(pl.multiple_of(kt * BQB, BQB), "full")
                return kt - 1, alive()

            _, top = lax.while_loop(cond, body, (qi - 1, alive()))

            @pl.when(top > SB_DEAD)
            def _():
                tile(meta_tile * BQB, "meta")

        lane_o = lax.broadcasted_iota(jnp.int32, (BQB, LANES), 1)
        g = g_ref[0].astype(f32)
        for pp in range(B_GROUP // 2):
            o = jnp.where(lane_o < B_DH, acc_sc[2 * pp], acc_sc[2 * pp + 1])
            cs = slice(pp * LANES, (pp + 1) * LANES)
            o_ref[0, :, cs] = (o * _silu(g[:, cs])).astype(o_ref.dtype)


def _attn_b(pe):
    bsz, lp, _ = pe.shape
    nq = lp // BQB
    w = (B_GROUP // 2) * LANES
    blk0 = 2 * A_QK + A_W
    qb, kb, vb, gb = [(blk0 + a * B_W) // w for a in range(4)]
    return pl.pallas_call(
        _attn_b_kernel,
        out_shape=jax.ShapeDtypeStruct((bsz, lp, B_W), bf16),
        grid=(bsz, B_HEADS // B_GROUP, nq),
        in_specs=[pl.BlockSpec((1, BQB, w), lambda b, p, i: (b, i, qb + p)),
                  pl.BlockSpec((1, lp, w), lambda b, p, i: (b, 0, kb + p)),
                  pl.BlockSpec((1, lp, w), lambda b, p, i: (b, 0, vb + p)),
                  pl.BlockSpec((1, BQB, w), lambda b, p, i: (b, i, gb + p))],
        out_specs=pl.BlockSpec((1, BQB, w), lambda b, p, i: (b, i, p)),
        scratch_shapes=[pltpu.VMEM((B_GROUP, BQB, 1), f32), pltpu.VMEM((B_GROUP, BQB, LANES), f32)],
        compiler_params=_cparams(("parallel", "parallel", "arbitrary")),
        name="attn_b",
    )(pe, pe, pe, pe)


C_GROUP = 4


def _attn_c_kernel(q_ref, k_ref, vt_ref, g_ref, o_ref, m_sc, l_sc, acc_sc, dmask_sc):
    qi = pl.program_id(2)

    @pl.when(qi == 0)
    def _():
        krow = lax.broadcasted_iota(jnp.int32, (BK, BQ), 0)
        qcol = lax.broadcasted_iota(jnp.int32, (BK, BQ), 1)
        dmask_sc[...] = jnp.where(_chunk_visible(qcol, krow), 0.0, NEG)

    q = q_ref[0]
    qs = [q[:, c * C_SLOT:(c + 1) * C_SLOT] for c in range(C_GROUP)]

    def keys(c, k0, n):
        return k_ref[0, pl.ds(k0, n), c * C_SLOT:(c + 1) * C_SLOT]

    def vals_t(c, kt):
        return vt_ref[kt, c * C_DV:(c + 1) * C_DV, :]

    def meta_scores(c):
        return _dot_nt(keys(c, MT - MS, MS), qs[c])

    def meta_vals_t(c):
        return vt_ref[MT // BK - 1, c * C_DV:(c + 1) * C_DV, BK - MS:]

    @pl.when(qi == 0)
    def _():
        _run_chains(C_GROUP, meta_scores,
                    lambda c, s: _softmax_block_t([(s, meta_vals_t(c))], m_sc, l_sc, acc_sc, c, True))

    @pl.when(qi >= 1)
    def _():
        k0 = pl.multiple_of(qi * BK, BK)
        _run_chains(C_GROUP, lambda c: (_dot_nt(keys(c, k0, BK), qs[c]), meta_scores(c)),
                    lambda c, s: _softmax_block_t([(s[0] + dmask_sc[...], vals_t(c, qi)), (s[1], meta_vals_t(c))],
                                                  m_sc, l_sc, acc_sc, c, True))

        def body(kt, carry):
            kk = pl.multiple_of(kt * BK, BK)
            _run_chains(C_GROUP, lambda c: _dot_nt(keys(c, kk, BK), qs[c]),
                        lambda c, s: _softmax_block_t([(s, vals_t(c, kt))], m_sc, l_sc, acc_sc, c, False))
            return carry
        lax.fori_loop(1, qi, body, 0)

    g = g_ref[0].astype(f32)
    for c in range(C_GROUP):
        o = (acc_sc[c] / l_sc[c]).T
        cs = slice(c * C_DV, (c + 1) * C_DV)
        o_ref[0, :, cs] = (o * _silu(g[:, cs])).astype(o_ref.dtype)


def _attn_c(qp, kp, vt, po):
    bsz, lp, _ = qp.shape
    nq = lp // BQ
    nk = lp // BK
    wq = C_GROUP * C_SLOT
    wv = C_GROUP * C_DV
    g_blk0 = ODD_G0 // wv
    return pl.pallas_call(
        _attn_c_kernel,
        out_shape=jax.ShapeDtypeStruct((bsz, lp, C_W), bf16),
        grid=(bsz, C_HEADS // C_GROUP, nq),
        in_specs=[pl.BlockSpec((1, BQ, wq), lambda b, h, i: (b, i, h)),
                  pl.BlockSpec((1, lp, wq), lambda b, h, i: (b, 0, h)),
                  pl.BlockSpec((nk, wv, BK), lambda b, h, i: (b, h, 0)),
                  pl.BlockSpec((1, BQ, wv), lambda b, h, i: (b, i, g_blk0 + h))],
        out_specs=pl.BlockSpec((1, BQ, wv), lambda b, h, i: (b, i, h)),
        scratch_shapes=[pltpu.VMEM((C_GROUP, 1, BQ), f32), pltpu.VMEM((C_GROUP, 1, BQ), f32),
                        pltpu.VMEM((C_GROUP, C_DV, BQ), f32), pltpu.VMEM((BK, BQ), f32)],
        compiler_params=_cparams(("parallel", "parallel", "arbitrary")),
        name="attn_c",
    )(qp, kp, vt, po)


def _rms(x, g):
    return x * lax.rsqrt(jnp.mean(x * x, axis=1, keepdims=True) + RMS_EPS) * g


def _odd_mid_kernel(cq_ref, ckv_ref, kpe_ref, cos_ref, sin_ref, kadd_ref, qn_ref, wq_ref, kvn_ref, wkn_ref,
                    wvt_ref, qp_ref, kp_ref, vt_ref):
    cos2 = cos_ref[...]
    sin2 = sin_ref[...]

    def rope(xx):
        return xx * cos2 + pltpu.roll(xx, C_ROPE, 1) * sin2

    lane = lax.broadcasted_iota(jnp.int32, (1, LANES), 1)
    qadd = jnp.where(lane == C_MASK_LANE, 1.0, 0.0)
    cq = _rms(cq_ref[...].astype(f32), qn_ref[...]).astype(bf16)
    qf = jnp.dot(cq, wq_ref[...], preferred_element_type=f32)
    for hh in range(C_HEADS):
        base = hh * C_SLOT
        qp_ref[:, base:base + C_NOPE] = qf[:, base:base + C_NOPE].astype(bf16)
        qp_ref[:, base + C_NOPE:base + C_SLOT] = (rope(qf[:, base + C_NOPE:base + C_SLOT]) + qadd).astype(bf16)

    ckv = _rms(ckv_ref[...].astype(f32), kvn_ref[...]).astype(bf16)
    kn = jnp.dot(ckv, wkn_ref[...], preferred_element_type=f32)
    kpe = (rope(kpe_ref[...].astype(f32)) + kadd_ref[...]).astype(bf16)
    for hh in range(C_HEADS):
        base = hh * C_SLOT
        kp_ref[:, base:base + C_NOPE] = kn[:, hh * C_NOPE:(hh + 1) * C_NOPE].astype(bf16)
        kp_ref[:, base + C_NOPE:base + C_SLOT] = kpe
    vt_ref[0] = _dot_nt(wvt_ref[...], ckv).astype(bf16)


def _odd_mid(po, tables, q_norm, wq, kv_norm, wkv):
    m = po.shape[0]
    bm = BK
    full = lambda a: pl.BlockSpec(a.shape, lambda i: (0, 0))
    qn = q_norm.reshape(1, -1)
    kvn = kv_norm.reshape(1, -1)
    wkv3 = wkv.reshape(KV_LORA, C_HEADS, C_NOPE + C_DV)
    wkn = wkv3[:, :, :C_NOPE].reshape(KV_LORA, C_HEADS * C_NOPE).astype(bf16)
    wvt = wkv3[:, :, C_NOPE:].reshape(KV_LORA, C_W).T.astype(bf16)
    tab = pl.BlockSpec((bm, LANES), lambda i: (i, 0))
    return pl.pallas_call(
        _odd_mid_kernel,
        out_shape=(jax.ShapeDtypeStruct((m, C_HEADS * C_SLOT), bf16),
                   jax.ShapeDtypeStruct((m, C_HEADS * C_SLOT), bf16),
                   jax.ShapeDtypeStruct((m // bm, C_W, bm), bf16)),
        grid=(m // bm,),
        in_specs=[pl.BlockSpec((bm, Q_LORA), lambda i: (i, ODD_CQ0 // Q_LORA)),
                  pl.BlockSpec((bm, KV_LORA), lambda i: (i, ODD_CKV0 // KV_LORA)),
                  pl.BlockSpec((bm, LANES), lambda i: (i, ODD_KPE0 // LANES)),
                  tab, tab, tab,
                  full(qn), full(wq), full(kvn), full(wkn), full(wvt)],
        out_specs=(pl.BlockSpec((bm, C_HEADS * C_SLOT), lambda i: (i, 0)),
                   pl.BlockSpec((bm, C_HEADS * C_SLOT), lambda i: (i, 0)),
                   pl.BlockSpec((1, C_W, bm), lambda i: (i, 0, 0))),
        compiler_params=_cparams(("parallel",)),
        name="odd_mid",
    )(po, po, po, *tables, qn, wq, kvn, wkn, wvt)


def _rot_cols(w):
    half = w.shape[-1] // 2
    return jnp.concatenate([-w[..., half:], w[..., :half]], axis=-1)


def _prep_even_w_in(w):
    qa = w[:, :A_QK] * (A_DH ** -0.5 * LOG2E)
    va0 = 2 * A_QK
    b0 = va0 + 2 * A_W
    qb = w[:, b0:b0 + B_W] * (B_DH ** -0.5)
    main = jnp.concatenate([qa, w[:, A_QK:va0], w[:, va0 + A_W:b0], qb, w[:, b0 + B_W:]], axis=1)
    return main.astype(bf16), w[:, va0:va0 + A_W].T.astype(bf16)


def _prep_odd_w_in(w):
    o1 = Q_LORA + KV_LORA
    kpe = w[:, o1:o1 + C_ROPE]
    return jnp.concatenate([w[:, o1 + C_ROPE:], w[:, :o1], kpe, _rot_cols(kpe)], axis=1).astype(bf16)


def _prep_w_uq(w):
    w3 = w.reshape(Q_LORA, C_HEADS, C_QK) * (C_QK ** -0.5 * LOG2E)
    rope_w = w3[:, :, C_NOPE:]
    return jnp.concatenate([w3[:, :, :C_NOPE], rope_w, _rot_cols(rope_w)], axis=2).reshape(
        Q_LORA, C_HEADS * C_SLOT).astype(bf16)


def _row_tables(lp, bsz):
    half = C_ROPE // 2
    u = jnp.arange(lp) - MT
    valid = u >= -N_META
    pos = jnp.where(valid, u + N_META, 0)
    inv_freq = ROPE_THETA ** (-jnp.arange(half, dtype=f32) / half)
    ang = pos.astype(f32)[:, None] * inv_freq[None, :]
    zeros = jnp.zeros((lp, LANES - C_ROPE), f32)
    cos = jnp.cos(ang)
    sin = jnp.sin(ang)
    kadd = jnp.where(valid[:, None] | (jnp.arange(LANES)[None, :] != C_MASK_LANE), 0.0, NEG).astype(f32)
    tabs = (jnp.concatenate([cos, cos, zeros], axis=1), jnp.concatenate([sin, sin, zeros], axis=1), kadd)
    return tuple(jnp.tile(t, (bsz, 1)) for t in tabs)


def kernel(x, meta, ln_g, ln_b, ev_w_in, ev_w_out, ev_lam_q1, ev_lam_k1, ev_lam_q2, ev_lam_k2,
           ev_subln, od_w_in, od_q_norm, od_w_uq, od_kv_norm, od_w_ukv, od_w_out):
    bsz, seq, d = x.shape
    assert d == D_MODEL and seq % BQ == 0
    lp = MT + seq
    rows = bsz * lp
    bm = 512

    h = jnp.concatenate([
        jnp.zeros((bsz, MT - N_META, d), x.dtype),
        jnp.broadcast_to(meta.astype(x.dtype)[None], (bsz, N_META, d)),
        x], axis=1).reshape(rows, d)

    slopes = jnp.asarray(2.0 ** (-8.0 * np.arange(1, A_HEADS + 1) / A_HEADS), dtype=f32)
    kbias = _alibi_key_columns(slopes, lp)
    tables = _row_tables(lp, bsz)

    for i in range(DEPTH):
        j = i // 2
        if i % 2 == 0:
            lam_init = 0.8 - 0.6 * math.exp(-0.3 * i)
            pe, vt = _even_proj(h, *_prep_even_w_in(ev_w_in[j]), EVEN_MAIN // 4)
            pe = pe.reshape(bsz, lp, EVEN_MAIN)
            ya = _attn_a(pe, vt, kbias, slopes, ev_lam_q1[j], ev_lam_k1[j], ev_lam_q2[j], ev_lam_k2[j],
                         ev_subln[j], lam_init)
            ys = _attn_b(pe)
            w_out = ev_w_out[j].astype(bf16)
            h = _post([ya.reshape(rows, A_W), ys.reshape(rows, B_W)], [w_out[:A_W], w_out[A_W:]],
                      h, ln_g[i], ln_b[i], bm)
        else:
            po = _matmul(h, _prep_odd_w_in(od_w_in[j]), f32, bm, ODD_IN_P // 3)
            qp, kp, vt = _odd_mid(po, tables, od_q_norm[j], _prep_w_uq(od_w_uq[j]),
                                  od_kv_norm[j], od_w_ukv[j])
            o = _attn_c(qp.reshape(bsz, lp, -1), kp.reshape(bsz, lp, -1), vt, po.reshape(bsz, lp, -1))
            h = _post([o.reshape(rows, C_W)], [od_w_out[j].astype(bf16)], h, ln_g[i], ln_b[i], bm)

    return h.reshape(bsz, lp, d)[:, MT:, :]
```
